```python
import jax, jax.numpy as jnp
from jax import lax
import numpy as np

D_MODEL = 1024
BATCH = 16
SEQ = 2048
DEPTH = 2
DEC_BATCH = 32
DEC_SEQ = 32
PAST_LEN = 4096

CHUNK = 64
EPS = 1e-6
GDN_HEADS = 4
GDN_DK = 128
GDN_DV = 128
CONV_WIDTH = 4
GDN_QKV = 2 * GDN_HEADS * GDN_DK + GDN_HEADS * GDN_DV
HGRN_HEADS = 4
HGRN_DK = 64
HGRN_DV = 64
HGRN_KW = HGRN_HEADS * HGRN_DK
MLA_HEADS = 4
Q_LORA = 256
KV_LORA = 128
QK_NOPE = 64
QK_ROPE = 32
V_HEAD = 64
ROPE_THETA = 10000.0
Q_BLOCK = 128
MIX_A = GDN_HEADS * GDN_DV
MIX_B = HGRN_HEADS * HGRN_DV
MIX_C = MLA_HEADS * V_HEAD
MIX_WIDTH = MIX_A + MIX_B + MIX_C
IN_SIZES = (GDN_QKV, MIX_A, GDN_HEADS, GDN_HEADS, HGRN_KW, MIX_B, HGRN_KW, MIX_B, Q_LORA, KV_LORA, QK_ROPE)
P_IN = GDN_QKV + MIX_A + 2 * GDN_HEADS + 2 * HGRN_KW + 2 * MIX_B + Q_LORA + KV_LORA + QK_ROPE
PEER_HEADS = 8
N_KEYS = 128
N_EXPERTS = N_KEYS * N_KEYS
PEER_DKEY = 256
PEER_TOPK = 16
TOKEN_BLOCK = 128

kernel_name = "hybrid_gdn_hgrn2_mla_peer_stream_step"

F32 = jnp.float32


def _offsets(sizes):
    out, acc = [], 0
    for s in sizes[:-1]:
        acc += s
        out.append(acc)
    return out


def rms_norm(x, g):
    xf = x.astype(F32)
    y = xf * lax.rsqrt(jnp.mean(xf * xf, axis=-1, keepdims=True) + EPS)
    return (y * g.astype(F32)).astype(x.dtype)


def l2_norm(x):
    xf = x.astype(F32)
    return (xf * lax.rsqrt(jnp.sum(xf * xf, axis=-1, keepdims=True) + EPS)).astype(x.dtype)


def rope(x, pos):
    half = QK_ROPE // 2
    inv = ROPE_THETA ** (-jnp.arange(half, dtype=F32) / half)
    ang = pos.astype(F32)[:, None] * inv[None, :]
    shp = (ang.shape[0],) + (1,) * (x.ndim - 3) + (half,)
    cos, sin = jnp.cos(ang).reshape(shp), jnp.sin(ang).reshape(shp)
    xf = x.astype(F32)
    x1, x2 = xf[..., :half], xf[..., half:]
    return jnp.concatenate([x1 * cos - x2 * sin, x1 * sin + x2 * cos], axis=-1).astype(x.dtype)


def causal_conv(x, buf, w):
    L = x.shape[1]
    xp = jnp.concatenate([buf.astype(x.dtype), x], axis=1)
    y = xp[:, 0:L] * w[0]
    for j in range(1, CONV_WIDTH):
        y = y + xp[:, j:j + L] * w[j]
    return y, xp[:, -(CONV_WIDTH - 1):]


def _to_chunks(t, n):
    B, L = t.shape[0], t.shape[1]
    pad = n * CHUNK - L
    t = jnp.pad(t, [(0, 0), (0, pad)] + [(0, 0)] * (t.ndim - 2))
    t = t.reshape((B, n, CHUNK) + t.shape[2:])
    perm = (1, 0, 3, 2) + tuple(range(4, t.ndim))
    return t.transpose(perm)


def _from_chunks(o, L):
    n, B, H, C, d = o.shape
    return o.transpose(1, 0, 3, 2, 4).reshape(B, n * C, H, d)[:, :L]


def gated_delta_chunked(q, k, v, g, beta, s0):
    L = q.shape[1]
    n = -(-L // CHUNK)
    qc = _to_chunks(q.astype(F32) * (GDN_DK ** -0.5), n)
    kc = _to_chunks(k.astype(F32), n)
    vc = _to_chunks(v.astype(F32), n)
    gc = jnp.cumsum(_to_chunks(g.astype(F32)[..., None], n)[..., 0], axis=-1)
    bc = _to_chunks(beta.astype(F32)[..., None], n)[..., 0]
    tri = jnp.tril(jnp.ones((CHUNK, CHUNK), bool))
    strict = jnp.tril(jnp.ones((CHUNK, CHUNK), bool), -1)
    diff = gc[..., :, None] - gc[..., None, :]
    decay = jnp.where(tri, jnp.exp(jnp.where(tri, diff, 0.0)), 0.0)
    kb = kc * bc[..., None]
    a_mat = jnp.where(strict, jnp.einsum('nbhtd,nbhsd->nbhts', kb, kc) * decay, 0.0)
    eye = jnp.eye(CHUNK, dtype=F32)
    t_inv = lax.linalg.triangular_solve(a_mat + eye, jnp.broadcast_to(eye, a_mat.shape),
                                        left_side=True, lower=True, unit_diagonal=True)
    u = jnp.einsum('nbhts,nbhsd->nbhtd', t_inv, vc * bc[..., None])
    w = jnp.einsum('nbhts,nbhsd->nbhtd', t_inv, kb * jnp.exp(gc)[..., None])

    def step(S, inp):
        q_i, k_i, u_i, w_i, g_i, dec_i = inp
        v_new = u_i - jnp.einsum('bhtd,bhde->bhte', w_i, S)
        o_inter = jnp.einsum('bhtd,bhde->bhte', q_i * jnp.exp(g_i)[..., None], S)
        attn = jnp.einsum('bhtd,bhsd->bhts', q_i, k_i) * dec_i
        o = o_inter + jnp.einsum('bhts,bhse->bhte', attn, v_new)
        g_last = g_i[..., -1]
        S = S * jnp.exp(g_last)[..., None, None] + jnp.einsum(
            'bhtd,bhte->bhde', k_i * jnp.exp(g_last[..., None] - g_i)[..., None], v_new)
        return S, o

    S, o = lax.scan(step, s0.astype(F32), (qc, kc, u, w, gc, decay))
    return _from_chunks(o, L), S


def hgrn2_chunked(q, k, v, log_f, s0):
    L = q.shape[1]
    n = -(-L // CHUNK)
    qc, kc, vc, fc = (_to_chunks(t.astype(F32), n) for t in (q, k, v, log_f))
    tri = jnp.tril(jnp.ones((CHUNK, CHUNK), bool))[:, :, None]

    def step(S, inp):
        q_i, k_i, v_i, lf = inp
        b = jnp.cumsum(lf, axis=2)
        dec = jnp.exp(jnp.where(tri, b[:, :, :, None, :] - b[:, :, None, :, :], -jnp.inf))
        attn = jnp.einsum('bhtd,bhtsd,bhsd->bhts', q_i, dec, k_i)
        o = jnp.einsum('bhts,bhse->bhte', attn, v_i) + jnp.einsum(
            'bhtd,bhde->bhte', q_i * jnp.exp(b), S)
        b_last = b[:, :, -1]
        S = jnp.exp(b_last)[..., None] * S + jnp.einsum(
            'bhtd,bhte->bhde', k_i * jnp.exp(b_last[:, :, None] - b), v_i)
        return S, o

    S, o = lax.scan(step, s0.astype(F32), (qc, kc, vc, fc))
    return _from_chunks(o, L), S


def chunk_causal_attention(q_nope, q_pe, k_nope, k_pe, v, q_pos, k_pos):
    B, Lq, H, _ = q_nope.shape
    qb = min(Q_BLOCK, Lq)
    nb = Lq // qb
    scale = (QK_NOPE + QK_ROPE) ** -0.5
    k_chunk = k_pos // CHUNK

    def block(args):
        qn, qp, qpos = args
        s = (jnp.einsum('bqhd,bkhd->bhqk', qn, k_nope)
             + jnp.einsum('bqhd,bkd->bhqk', qp, k_pe)).astype(F32) * scale
        mask = k_chunk[None, :] <= (qpos // CHUNK)[:, None]
        p = jax.nn.softmax(jnp.where(mask, s, -jnp.inf), axis=-1).astype(v.dtype)
        return jnp.einsum('bhqk,bkhd->bqhd', p, v)

    qn_b = q_nope.reshape(B, nb, qb, H, QK_NOPE).transpose(1, 0, 2, 3, 4)
    qp_b = q_pe.reshape(B, nb, qb, H, QK_ROPE).transpose(1, 0, 2, 3, 4)
    out = lax.map(block, (qn_b, qp_b, q_pos.reshape(nb, qb)))
    return out.transpose(1, 0, 2, 3, 4).reshape(B, Lq, H, V_HEAD)


def peer(x, w_query, sub_keys, expert_u, expert_v):
    B, L, D = x.shape
    T = B * L
    nb = -(-T // TOKEN_BLOCK)
    t = jnp.pad(x.reshape(T, D), ((0, nb * TOKEN_BLOCK - T), (0, 0))).reshape(nb, TOKEN_BLOCK, D)

    def block(xb):
        q = (xb @ w_query).reshape(TOKEN_BLOCK, PEER_HEADS, 2, PEER_DKEY // 2)
        s = jnp.einsum('thpd,hpnd->thpn', q, sub_keys).astype(F32)
        s_top, i_top = lax.top_k(s, PEER_TOPK)
        cand = s_top[:, :, 0, :, None] + s_top[:, :, 1, None, :]
        cidx = i_top[:, :, 0, :, None] * N_KEYS + i_top[:, :, 1, None, :]
        best, sel = lax.top_k(cand.reshape(TOKEN_BLOCK, PEER_HEADS, PEER_TOPK * PEER_TOPK), PEER_TOPK)
        eidx = jnp.take_along_axis(cidx.reshape(TOKEN_BLOCK, PEER_HEADS, -1), sel, axis=-1)
        gate = jax.nn.softmax(best, axis=-1).reshape(TOKEN_BLOCK, PEER_HEADS * PEER_TOPK)
        eidx = eidx.reshape(TOKEN_BLOCK, PEER_HEADS * PEER_TOPK)
        act = jax.nn.gelu(jnp.einsum('td,ted->te', xb, expert_u[eidx]), approximate=False)
        return jnp.einsum('te,ted->td', gate.astype(xb.dtype) * act, expert_v[eidx])

    y = lax.map(block, t).reshape(nb * TOKEN_BLOCK, D)[:T]
    return y.reshape(B, L, D)


def trunk_layer(x, pos, past_pos, past_ckv, past_kpe, gdn_s0, conv_buf, hgrn_s0, lb, p):
    B, L, _ = x.shape
    h = rms_norm(x, p['norm_mix'])
    (a_qkv, a_z, a_beta, a_alpha, b_f, b_i, b_q, b_g,
     c_qa, c_kva, c_kpe) = jnp.split(h @ p['w_in'], _offsets(IN_SIZES), axis=-1)

    conv, new_conv = causal_conv(a_qkv, conv_buf, p['conv_w'])
    conv = jax.nn.silu(conv)
    aq, ak, av = jnp.split(conv, [GDN_HEADS * GDN_DK, 2 * GDN_HEADS * GDN_DK], axis=-1)
    aq = l2_norm(aq.reshape(B, L, GDN_HEADS, GDN_DK))
    ak = l2_norm(ak.reshape(B, L, GDN_HEADS, GDN_DK))
    av = av.reshape(B, L, GDN_HEADS, GDN_DV)
    beta = jax.nn.sigmoid(a_beta.astype(F32))
    g = -jnp.exp(p['gdn_a_log'].astype(F32)) * jax.nn.softplus(a_alpha.astype(F32) + p['gdn_dt_bias'].astype(F32))
    o_a, new_gdn = gated_delta_chunked(aq, ak, av, g, beta, gdn_s0)
    o_a = rms_norm(o_a.astype(x.dtype), p['gdn_norm']) * jax.nn.silu(a_z.reshape(B, L, GDN_HEADS, GDN_DV))

    zf = b_f.astype(F32).reshape(B, L, HGRN_HEADS, HGRN_DK)
    lbh = lb.reshape(HGRN_HEADS, HGRN_DK)
    log_f = jnp.logaddexp(jnp.log(lbh), jnp.log1p(-lbh) + jax.nn.log_sigmoid(zf))
    k_b = (1.0 - lbh) * jax.nn.sigmoid(-zf)
    q_b = jax.nn.silu(b_q).reshape(B, L, HGRN_HEADS, HGRN_DK)
    o_b, new_hgrn = hgrn2_chunked(q_b, k_b, b_i.reshape(B, L, HGRN_HEADS, HGRN_DV), log_f, hgrn_s0)
    o_b = rms_norm(o_b.astype(x.dtype), p['hgrn_norm']) * jax.nn.silu(b_g.reshape(B, L, HGRN_HEADS, HGRN_DV))

    qc = (rms_norm(c_qa, p['mla_q_a_norm']) @ p['mla_w_q_b']).reshape(B, L, MLA_HEADS, QK_NOPE + QK_ROPE)
    q_nope = rms_norm(qc[..., :QK_NOPE], p['mla_q_norm_nope'])
    q_pe = rope(rms_norm(qc[..., QK_NOPE:], p['mla_q_norm_rope']), pos)
    ckv = rms_norm(c_kva, p['mla_kv_a_norm'])
    kpe = rope(rms_norm(c_kpe, p['mla_k_norm_rope']), pos)
    all_ckv = jnp.concatenate([past_ckv.astype(x.dtype), ckv], axis=1)
    all_kpe = jnp.concatenate([past_kpe.astype(x.dtype), kpe], axis=1)
    k_pos = jnp.concatenate([past_pos, pos])
    kv = (all_ckv @ p['mla_w_kv_b']).reshape(B, all_ckv.shape[1], MLA_HEADS, QK_NOPE + V_HEAD)
    k_nope = rms_norm(kv[..., :QK_NOPE], p['mla_k_norm_nope'])
    o_c = chunk_causal_attention(q_nope, q_pe, k_nope, all_kpe, kv[..., QK_NOPE:], pos, k_pos)

    mixed = jnp.concatenate([o_a.reshape(B, L, MIX_A), o_b.reshape(B, L, MIX_B),
                             o_c.reshape(B, L, MIX_C)], axis=-1) @ p['w_out']
    x = x + mixed
    x = x + peer(rms_norm(x, p['norm_ffn']), p['peer_w_query'], p['peer_sub_keys'],
                 p['peer_expert_u'], p['peer_expert_v'])
    return x, new_gdn, new_conv, new_hgrn, ckv, kpe


def setup_inputs(seed: int = 0) -> dict:
    key = jax.random.key(seed)
    ks = jax.random.split(key, 32)

    def nrm(k, shape, s):
        return jax.random.normal(k, shape, F32) * s

    def gain(k, shape):
        return 1.0 + 0.01 * jax.random.normal(k, shape, F32)

    return {
        'x_prompt': nrm(ks[0], (BATCH, SEQ, D_MODEL), 1.0),
        'x_sample': nrm(ks[1], (DEC_BATCH, DEC_SEQ, D_MODEL), 1.0),
        'state_gdn': nrm(ks[2], (DEPTH, DEC_BATCH, GDN_HEADS, GDN_DK, GDN_DV), 0.5),
        'state_gdn_conv': nrm(ks[3], (DEPTH, DEC_BATCH, CONV_WIDTH - 1, GDN_QKV), 1.0),
        'state_hgrn': nrm(ks[4], (DEPTH, DEC_BATCH, HGRN_HEADS, HGRN_DK, HGRN_DV), 0.5),
        'cache_mla_ckv': nrm(ks[5], (DEPTH, DEC_BATCH, PAST_LEN, KV_LORA), 1.0),
        'cache_mla_kpe': nrm(ks[6], (DEPTH, DEC_BATCH, PAST_LEN, QK_ROPE), 1.0),
        'norm_mix': gain(ks[7], (DEPTH, D_MODEL)),
        'w_in': nrm(ks[8], (DEPTH, D_MODEL, P_IN), D_MODEL ** -0.5),
        'conv_w': nrm(ks[9], (DEPTH, CONV_WIDTH, GDN_QKV), 0.5),
        'gdn_a_log': jnp.log(jax.random.uniform(ks[10], (DEPTH, GDN_HEADS), F32, 1.0, 16.0)),
        'gdn_dt_bias': nrm(ks[11], (DEPTH, GDN_HEADS), 0.1),
        'gdn_norm': gain(ks[12], (DEPTH, GDN_DV)),
        'hgrn_lower_bounds': nrm(ks[13], (DEPTH, HGRN_KW), 0.1),
        'hgrn_norm': gain(ks[14], (DEPTH, HGRN_DV)),
        'mla_q_a_norm': gain(ks[15], (DEPTH, Q_LORA)),
        'mla_w_q_b': nrm(ks[16], (DEPTH, Q_LORA, MLA_HEADS * (QK_NOPE + QK_ROPE)), Q_LORA ** -0.5),
        'mla_kv_a_norm': gain(ks[17], (DEPTH, KV_LORA)),
        'mla_w_kv_b': nrm(ks[18], (DEPTH, KV_LORA, MLA_HEADS * (QK_NOPE + V_HEAD)), KV_LORA ** -0.5),
        'mla_q_norm_nope': gain(ks[19], (DEPTH, QK_NOPE)),
        'mla_q_norm_rope': gain(ks[20], (DEPTH, QK_ROPE)),
        'mla_k_norm_nope': gain(ks[21], (DEPTH, QK_NOPE)),
        'mla_k_norm_rope': gain(ks[22], (DEPTH, QK_ROPE)),
        'w_out': nrm(ks[23], (DEPTH, MIX_WIDTH, D_MODEL), MIX_WIDTH ** -0.5),
        'norm_ffn': gain(ks[24], (DEPTH, D_MODEL)),
        'peer_w_query': nrm(ks[25], (DEPTH, D_MODEL, PEER_HEADS * PEER_DKEY), D_MODEL ** -0.5),
        'peer_sub_keys': nrm(ks[26], (DEPTH, PEER_HEADS, 2, N_KEYS, PEER_DKEY // 2), (PEER_DKEY // 2) ** -0.5),
        'peer_expert_u': nrm(ks[27], (DEPTH, N_EXPERTS, D_MODEL), D_MODEL ** -0.5),
        'peer_expert_v': nrm(ks[28], (DEPTH, N_EXPERTS, D_MODEL), 0.25),
    }


def reference(x_prompt, x_sample, state_gdn, state_gdn_conv, state_hgrn, cache_mla_ckv, cache_mla_kpe,
              norm_mix, w_in, conv_w, gdn_a_log, gdn_dt_bias, gdn_norm, hgrn_lower_bounds, hgrn_norm,
              mla_q_a_norm, mla_w_q_b, mla_kv_a_norm, mla_w_kv_b, mla_q_norm_nope, mla_q_norm_rope,
              mla_k_norm_nope, mla_k_norm_rope, w_out, norm_ffn, peer_w_query, peer_sub_keys,
              peer_expert_u, peer_expert_v):
    dt = x_prompt.dtype
    Bp, Lp = x_prompt.shape[0], x_prompt.shape[1]
    Ls = x_sample.shape[1]
    past_len = cache_mla_ckv.shape[2]
    pos_p = jnp.arange(Lp, dtype=jnp.int32)
    pos_s = past_len + jnp.arange(Ls, dtype=jnp.int32)
    past_pos_s = jnp.arange(past_len, dtype=jnp.int32)
    past_pos_p = jnp.zeros((0,), jnp.int32)
    lb_all = jnp.cumsum(jax.nn.softmax(hgrn_lower_bounds.astype(F32), axis=0), axis=0)
    lb_all = lb_all - lb_all[0:1]

    yp, ys = x_prompt, x_sample
    pg, pc, ph, pk, pr = [], [], [], [], []
    sg, sc, sh, sk, sr = [], [], [], [], []
    for l in range(DEPTH):
        p = {'norm_mix': norm_mix[l], 'w_in': w_in[l], 'conv_w': conv_w[l], 'gdn_a_log': gdn_a_log[l],
             'gdn_dt_bias': gdn_dt_bias[l], 'gdn_norm': gdn_norm[l], 'hgrn_norm': hgrn_norm[l],
             'mla_q_a_norm': mla_q_a_norm[l], 'mla_w_q_b': mla_w_q_b[l], 'mla_kv_a_norm': mla_kv_a_norm[l],
             'mla_w_kv_b': mla_w_kv_b[l], 'mla_q_norm_nope': mla_q_norm_nope[l],
             'mla_q_norm_rope': mla_q_norm_rope[l], 'mla_k_norm_nope': mla_k_norm_nope[l],
             'mla_k_norm_rope': mla_k_norm_rope[l], 'w_out': w_out[l], 'norm_ffn': norm_ffn[l],
             'peer_w_query': peer_w_query[l], 'peer_sub_keys': peer_sub_keys[l],
             'peer_expert_u': peer_expert_u[l], 'peer_expert_v': peer_expert_v[l]}
        yp, g_p, c_p, h_p, k_p, r_p = trunk_layer(
            yp, pos_p, past_pos_p,
            jnp.zeros((Bp, 0, KV_LORA), dt), jnp.zeros((Bp, 0, QK_ROPE), dt),
            jnp.zeros((Bp, GDN_HEADS, GDN_DK, GDN_DV), F32), jnp.zeros((Bp, CONV_WIDTH - 1, GDN_QKV), dt),
            jnp.zeros((Bp, HGRN_HEADS, HGRN_DK, HGRN_DV), F32), lb_all[l], p)
        ys, g_s, c_s, h_s, k_s, r_s = trunk_layer(
            ys, pos_s, past_pos_s, cache_mla_ckv[l], cache_mla_kpe[l],
            state_gdn[l], state_gdn_conv[l], state_hgrn[l], lb_all[l], p)
        pg.append(g_p.astype(dt)); pc.append(c_p.astype(dt)); ph.append(h_p.astype(dt))
        pk.append(k_p.astype(dt)); pr.append(r_p.astype(dt))
        sg.append(g_s.astype(dt)); sc.append(c_s.astype(dt)); sh.append(h_s.astype(dt))
        sk.append(k_s.astype(dt)); sr.append(r_s.astype(dt))

    return (yp, ys,
            jnp.stack(pg), jnp.stack(pc), jnp.stack(ph), jnp.stack(pk), jnp.stack(pr),
            jnp.stack(sg), jnp.stack(sc), jnp.stack(sh), jnp.stack(sk), jnp.stack(sr))
```

```python
import functools

import jax
import jax.numpy as jnp
from jax import lax
from jax.experimental import pallas as pl
from jax.experimental.pallas import tpu as pltpu

F32 = jnp.float32
BF16 = jnp.bfloat16
I32 = jnp.int32

EPS = 1e-6
CHUNK = 64
GDN_HEADS, GDN_DK, GDN_DV = 4, 128, 128
CONV_WIDTH = 4
HGRN_HEADS, HGRN_DK, HGRN_DV = 4, 64, 64
HGRN_SUB = 16
MLA_HEADS = 4
Q_LORA, KV_LORA, QK_NOPE, QK_ROPE, V_HEAD = 256, 128, 64, 32, 64
ROPE_THETA = 10000.0
PEER_HEADS, N_KEYS, PEER_TOPK = 8, 128, 16
PEER_SEL = PEER_HEADS * PEER_TOPK

GDN_QKV = 2 * GDN_HEADS * GDN_DK + GDN_HEADS * GDN_DV
MIX_A = GDN_HEADS * GDN_DV
HGRN_KW = HGRN_HEADS * HGRN_DK
MIX_B = HGRN_HEADS * HGRN_DV
MIX_C = MLA_HEADS * V_HEAD

LANES = 128
SUBLANES = 8
VMEM_LIMIT_BYTES = 48 * 1024 * 1024

_COL_QKV, _COL_Z, _COL_HGRN, _COL_MLA, _COL_BA = 0, 1536, 2048, 3072, 3584
_MLA_W = 512
_P_PAD = 3712

NEG_BIG = -1e30


def _cparams(*sem):
    return pltpu.CompilerParams(dimension_semantics=sem, vmem_limit_bytes=VMEM_LIMIT_BYTES)


def _bdot(a, b):
    return jnp.dot(a.astype(BF16), b.astype(BF16), preferred_element_type=F32)


def _bdot_nt(a, b):
    return lax.dot_general(a.astype(BF16), b.astype(BF16), (((1,), (1,)), ((), ())), preferred_element_type=F32)


def _bdot_tn(a, b):
    return lax.dot_general(a.astype(BF16), b.astype(BF16), (((0,), (0,)), ((), ())), preferred_element_type=F32)


def _hdot(a, b):
    return jnp.dot(a, b, precision=lax.Precision.HIGHEST, preferred_element_type=F32)


def _dot_exact_rhs(x, rhs_bf16):
    hi = x.astype(BF16)
    lo = (x - hi.astype(F32)).astype(BF16)
    return (jnp.dot(hi, rhs_bf16, preferred_element_type=F32) + jnp.dot(lo, rhs_bf16, preferred_element_type=F32))


def _block_ones(n, seg):
    r = lax.broadcasted_iota(I32, (n, n), 0) // seg
    c = lax.broadcasted_iota(I32, (n, n), 1) // seg
    return (r == c).astype(BF16)


def _rms(x, g):
    return x * lax.rsqrt(jnp.mean(x * x, axis=-1, keepdims=True) + EPS) * g


def _seg_rms(x, g, seg, ones):
    ss = _dot_exact_rhs(x * x, ones) * (1.0 / seg)
    return x * lax.rsqrt(ss + EPS) * g


def _softplus(x):
    return jnp.maximum(x, 0.0) + jnp.log1p(jnp.exp(-jnp.abs(x)))


def _cumsum_rows(x):
    n = x.shape[0]
    row = lax.broadcasted_iota(I32, x.shape, 0)
    s = 1
    while s < n:
        x = x + jnp.where(row >= s, pltpu.roll(x, s, 0), 0.0)
        s *= 2
    return x


def _norm_proj_kernel(x_ref, g_ref, w_ref, o_ref):
    h = _rms(x_ref[...], g_ref[...])
    o_ref[...] = jnp.dot(h.astype(BF16), w_ref[...], preferred_element_type=F32)


def _norm_proj(x2d, g, w_bf16, tm):
    t, d = x2d.shape
    n = w_bf16.shape[1]
    return pl.pallas_call(
        _norm_proj_kernel,
        grid=(t // tm,),
        in_specs=[pl.BlockSpec((tm, d), lambda i: (i, 0)),
                  pl.BlockSpec((1, d), lambda i: (0, 0)),
                  pl.BlockSpec((d, n), lambda i: (0, 0))],
        out_specs=pl.BlockSpec((tm, n), lambda i: (i, 0)),
        out_shape=jax.ShapeDtypeStruct((t, n), F32),
        compiler_params=_cparams("parallel"),
        name="in_proj",
    )(x2d, g, w_bf16)


def _unit_lower_inverse(a):
    c = a.shape[0]
    r = lax.broadcasted_iota(I32, (c, c), 0)
    q = lax.broadcasted_iota(I32, (c, c), 1)
    eye = (r == q).astype(F32)
    blk = (r // 16) == (q // 16)
    d = jnp.where(blk, a, 0.0)
    e = a - d
    d2 = _hdot(d, d)
    d4 = _hdot(d2, d2)
    d8 = _hdot(d4, d4)
    td = _hdot(_hdot(_hdot(eye - d, eye + d2), eye + d4), eye + d8)
    n = _hdot(td, e)
    n2 = _hdot(n, n)
    return _hdot(_hdot(eye - n, eye + n2), td)


def _gdn_kernel(qkv_ref, z_ref, ba_ref, conv0_ref, convw_ref, gp_ref, s0_ref, nw_ref,
                o_ref, convn_ref, sn_ref, xp_ref, s_ref, *, valid):
    c = pl.program_id(1)
    last = c == pl.num_programs(1) - 1
    C, H, DK, DV = CHUNK, GDN_HEADS, GDN_DK, GDN_DV

    @pl.when(c == 0)
    def _():
        xp_ref[0:8, :] = conv0_ref[...]
        s_ref[...] = s0_ref[...]

    xp_ref[8:8 + C, :] = qkv_ref[...]
    y = convw_ref[0:1, :] * xp_ref[5:5 + C, :]
    for j in range(1, CONV_WIDTH):
        y = y + convw_ref[j:j + 1, :] * xp_ref[5 + j:5 + j + C, :]

    @pl.when(last)
    def _():
        convn_ref[0:3, :] = xp_ref[5 + valid:8 + valid, :]
        convn_ref[3:8, :] = jnp.zeros((5, convn_ref.shape[1]), F32)

    xp_ref[5:8, :] = xp_ref[5 + C:8 + C, :]

    y = y * jax.nn.sigmoid(y)
    ba = ba_ref[...]
    beta = jax.nn.sigmoid(ba)
    g = -jnp.exp(gp_ref[0:1, :]) * _softplus(ba + gp_ref[1:2, :])
    if valid < C:
        rows = lax.broadcasted_iota(I32, (C, 1), 0)
        keep = jnp.logical_or(rows < valid, jnp.logical_not(last))
        y = jnp.where(keep, y, 0.0)
        beta = jnp.where(keep, beta, 0.0)
        g = jnp.where(keep, g, 0.0)
    gc = _cumsum_rows(g)
    gct = gc.T
    r = lax.broadcasted_iota(I32, (C, C), 0)
    q_ = lax.broadcasted_iota(I32, (C, C), 1)
    tri = r >= q_
    strict = r > q_

    for h in range(H):
        qh = y[:, h * DK:(h + 1) * DK]
        kh = y[:, H * DK + h * DK:H * DK + (h + 1) * DK]
        vh = y[:, 2 * H * DK + h * DV:2 * H * DK + (h + 1) * DV]
        qh = qh * lax.rsqrt(jnp.sum(qh * qh, axis=-1, keepdims=True) + EPS) * (DK ** -0.5)
        kh = kh * lax.rsqrt(jnp.sum(kh * kh, axis=-1, keepdims=True) + EPS)
        gcol = gc[:, H + h:H + h + 1]
        grow = gct[H + h:H + h + 1, :]
        bcol = beta[:, h:h + 1]
        decay = jnp.where(tri, jnp.exp(jnp.where(tri, gcol - grow, 0.0)), 0.0)
        kb = kh * bcol
        a_mat = jnp.where(strict, _bdot_nt(kb, kh) * decay, 0.0)
        t_inv = _unit_lower_inverse(a_mat)
        u = _bdot(t_inv, vh * bcol)
        w = _bdot(t_inv, kb * jnp.exp(gcol))
        s_h = s_ref[h]
        v_new = u - _bdot(w, s_h)
        o_h = _bdot(qh * jnp.exp(gcol), s_h) + _bdot(_bdot_nt(qh, kh) * decay, v_new)
        glast = gc[C - 1:C, H + h:H + h + 1]
        s_ref[h] = s_h * jnp.exp(glast) + _bdot_tn(kh * jnp.exp(glast - gcol), v_new)
        zh = z_ref[:, h * DV:(h + 1) * DV]
        o_ref[:, h * DV:(h + 1) * DV] = _rms(o_h, nw_ref[...]) * (zh * jax.nn.sigmoid(zh))

    @pl.when(last)
    def _():
        sn_ref[...] = s_ref[...]


def _gdn(proj3, conv0, convw, gp, s0, nw, valid):
    b, lp, _ = proj3.shape
    nc = lp // CHUNK
    H, DK, DV = GDN_HEADS, GDN_DK, GDN_DV
    return pl.pallas_call(
        functools.partial(_gdn_kernel, valid=valid),
        grid=(b, nc),
        in_specs=[pl.BlockSpec((None, CHUNK, GDN_QKV), lambda i, c: (i, c, _COL_QKV // GDN_QKV)),
                  pl.BlockSpec((None, CHUNK, MIX_A), lambda i, c: (i, c, _COL_Z // MIX_A)),
                  pl.BlockSpec((None, CHUNK, LANES), lambda i, c: (i, c, _COL_BA // LANES)),
                  pl.BlockSpec((None, 8, GDN_QKV), lambda i, c: (i, 0, 0)),
                  pl.BlockSpec((8, GDN_QKV), lambda i, c: (0, 0)),
                  pl.BlockSpec((8, LANES), lambda i, c: (0, 0)),
                  pl.BlockSpec((None, H, DK, DV), lambda i, c: (i, 0, 0, 0)),
                  pl.BlockSpec((1, DV), lambda i, c: (0, 0))],
        out_specs=[pl.BlockSpec((None, CHUNK, MIX_A), lambda i, c: (i, c, 0)),
                   pl.BlockSpec((None, 8, GDN_QKV), lambda i, c: (i, 0, 0)),
                   pl.BlockSpec((None, H, DK, DV), lambda i, c: (i, 0, 0, 0))],
        out_shape=[jax.ShapeDtypeStruct((b, lp, MIX_A), F32),
                   jax.ShapeDtypeStruct((b, 8, GDN_QKV), F32),
                   jax.ShapeDtypeStruct((b, H, DK, DV), F32)],
        scratch_shapes=[pltpu.VMEM((8 + CHUNK, GDN_QKV), F32), pltpu.VMEM((H, DK, DV), F32)],
        compiler_params=_cparams("parallel", "arbitrary"),
        name="gdn",
    )(proj3, proj3, proj3, conv0, convw, gp, s0, nw)


def _hgrn_kernel(hg_ref, lb_ref, s0_ref, nw_ref, o_ref, sn_ref, st_ref, *, valid):
    c = pl.program_id(1)
    last = c == pl.num_programs(1) - 1
    C, H, DK, W, SUB = CHUNK, HGRN_HEADS, HGRN_DK, HGRN_KW, HGRN_SUB

    @pl.when(c == 0)
    def _():
        st_ref[...] = s0_ref[...]

    zf = hg_ref[:, 0:W]
    v = hg_ref[:, W:2 * W]
    bq = hg_ref[:, 2 * W:3 * W]
    bg = hg_ref[:, 3 * W:4 * W]
    lb = lb_ref[...]
    log_sig = -_softplus(-zf)
    a = jnp.log(lb)
    b2 = jnp.log1p(-lb) + log_sig
    log_f = jnp.maximum(a, b2) + jnp.log1p(jnp.exp(-jnp.abs(a - b2)))
    kk = (1.0 - lb) * jax.nn.sigmoid(-zf)
    qq = bq * jax.nn.sigmoid(bq)
    rows = lax.broadcasted_iota(I32, (C, W), 0)
    if valid < C:
        keep = jnp.logical_or(rows < valid, jnp.logical_not(last))
        log_f = jnp.where(keep, log_f, 0.0)
        kk = jnp.where(keep, kk, 0.0)
        qq = jnp.where(keep, qq, 0.0)
        v = jnp.where(keep, v, 0.0)
    b = _cumsum_rows(log_f)
    blast = b[C - 1:C, :]
    ones = _block_ones(W, DK)
    lane_head = lax.broadcasted_iota(I32, (1, W), 1) // DK

    st = st_ref[...]
    o = _bdot_nt(qq * jnp.exp(b), st)

    sub_pos = rows % SUB
    for delta in range(SUB):
        m = sub_pos >= delta
        kr = pltpu.roll(kk, delta, 0) if delta else kk
        br = pltpu.roll(b, delta, 0) if delta else b
        vr = pltpu.roll(v, delta, 0) if delta else v
        dm = jnp.where(m, qq * kr * jnp.exp(jnp.where(m, b - br, 0.0)), 0.0)
        o = o + _bdot(dm, ones) * vr

    pieces = [jnp.zeros((SUB, W), F32)]
    for i in range(1, C // SUB):
        lo = i * SUB
        ri = b[lo:lo + 1, :]
        q_i = qq[lo:lo + SUB, :] * jnp.exp(b[lo:lo + SUB, :] - ri)
        k_i = kk[0:lo, :] * jnp.exp(ri - b[0:lo, :])
        q_st = jnp.concatenate([jnp.where(lane_head == h, q_i, 0.0) for h in range(H)], axis=0)
        res = _bdot(_bdot_nt(q_st, k_i), v[0:lo, :])
        acc = jnp.where(lane_head == 0, res[0:SUB, :], 0.0)
        for h in range(1, H):
            acc = acc + jnp.where(lane_head == h, res[h * SUB:(h + 1) * SUB, :], 0.0)
        pieces.append(acc)
    o = o + jnp.concatenate(pieces, axis=0)

    upd = _bdot_tn(v, kk * jnp.exp(blast - b))
    rr = lax.broadcasted_iota(I32, (W, W), 0) // DK
    cc = lax.broadcasted_iota(I32, (W, W), 1) // DK
    st_new = st * jnp.exp(blast) + jnp.where(rr == cc, upd, 0.0)
    st_ref[...] = st_new

    o_ref[...] = _seg_rms(o, nw_ref[...], HGRN_DV, ones) * (bg * jax.nn.sigmoid(bg))

    @pl.when(last)
    def _():
        sn_ref[...] = st_new


def _hgrn(proj3, lb, st0, nw, valid):
    b, lp, _ = proj3.shape
    nc = lp // CHUNK
    W = HGRN_KW
    return pl.pallas_call(
        functools.partial(_hgrn_kernel, valid=valid),
        grid=(b, nc),
        in_specs=[pl.BlockSpec((None, CHUNK, 4 * W), lambda i, c: (i, c, _COL_HGRN // (4 * W))),
                  pl.BlockSpec((1, W), lambda i, c: (0, 0)),
                  pl.BlockSpec((None, W, W), lambda i, c: (i, 0, 0)),
                  pl.BlockSpec((1, W), lambda i, c: (0, 0))],
        out_specs=[pl.BlockSpec((None, CHUNK, W), lambda i, c: (i, c, 0)),
                   pl.BlockSpec((None, W, W), lambda i, c: (i, 0, 0))],
        out_shape=[jax.ShapeDtypeStruct((b, lp, W), F32),
                   jax.ShapeDtypeStruct((b, W, W), F32)],
        scratch_shapes=[pltpu.VMEM((W, W), F32)],
        compiler_params=_cparams("parallel", "arbitrary"),
        name="hgrn",
    )(proj3, lb, st0, nw)


def _rope_lanes(x, cos, sin):
    half = QK_ROPE // 2
    lane = lax.broadcasted_iota(I32, x.shape, 1) % QK_ROPE
    rot = jnp.where(lane < half, -pltpu.roll(x, x.shape[1] - half, 1), pltpu.roll(x, half, 1))
    return x * cos + rot * sin


def _tile_rope_lanes(x):
    out = x
    for k in range(1, LANES // QK_ROPE):
        out = out + pltpu.roll(x, k * QK_ROPE, 1)
    return out


def _mla_prep_kernel(m_ref, cos_ref, sin_ref, wq_ref, gqa_ref, gqn_ref, gqr_ref, gkv_ref, gkr_ref,
                     q_ref, ckv_ref, kpe_ref):
    NW = MLA_HEADS * QK_NOPE
    qa = m_ref[:, 0:Q_LORA]
    kva = m_ref[:, Q_LORA:Q_LORA + KV_LORA]
    kp = m_ref[:, Q_LORA + KV_LORA:Q_LORA + KV_LORA + LANES]
    cos, sin = cos_ref[...], sin_ref[...]
    qc = jnp.dot(_rms(qa, gqa_ref[...]).astype(BF16), wq_ref[...], preferred_element_type=F32)
    qn = _seg_rms(qc[:, 0:NW], gqn_ref[...], QK_NOPE, _block_ones(NW, QK_NOPE))
    qp = _seg_rms(qc[:, NW:NW + LANES], gqr_ref[...], QK_ROPE, _block_ones(LANES, QK_ROPE))
    qp = _rope_lanes(qp, cos, sin)
    q_ref[:, 0:NW] = qn.astype(BF16)
    q_ref[:, NW:NW + LANES] = qp.astype(BF16)
    ckv_ref[...] = _rms(kva, gkv_ref[...])
    kn = kp * lax.rsqrt(jnp.sum(kp * kp, axis=-1, keepdims=True) * (1.0 / QK_ROPE) + EPS) * gkr_ref[...]
    kpe_ref[...] = _rope_lanes(_tile_rope_lanes(kn), cos, sin)


def _mla_prep(proj, cos, sin, wq, gqa, gqn, gqr, gkv, gkr, tm):
    t = proj.shape[0]
    nrep = cos.shape[0] // tm
    qw = MLA_HEADS * QK_NOPE + LANES
    full = lambda i: (0, 0)
    return pl.pallas_call(
        _mla_prep_kernel,
        grid=(t // tm,),
        in_specs=[pl.BlockSpec((tm, _MLA_W), lambda i: (i, _COL_MLA // _MLA_W)),
                  pl.BlockSpec((tm, LANES), lambda i: (i % nrep, 0)),
                  pl.BlockSpec((tm, LANES), lambda i: (i % nrep, 0)),
                  pl.BlockSpec(wq.shape, full),
                  pl.BlockSpec(gqa.shape, full), pl.BlockSpec(gqn.shape, full), pl.BlockSpec(gqr.shape, full),
                  pl.BlockSpec(gkv.shape, full), pl.BlockSpec(gkr.shape, full)],
        out_specs=[pl.BlockSpec((tm, qw), lambda i: (i, 0)),
                   pl.BlockSpec((tm, KV_LORA), lambda i: (i, 0)),
                   pl.BlockSpec((tm, LANES), lambda i: (i, 0))],
        out_shape=[jax.ShapeDtypeStruct((t, qw), BF16),
                   jax.ShapeDtypeStruct((t, KV_LORA), F32),
                   jax.ShapeDtypeStruct((t, LANES), F32)],
        compiler_params=_cparams("parallel"),
        name="mla_prep",
    )(proj, cos, sin, wq, gqa, gqn, gqr, gkv, gkr)


def _kv_prep_kernel(ckv_ref, kpe_ref, wkv_ref, gkn_ref, k_ref, v_ref):
    NW = MLA_HEADS * QK_NOPE
    kv = jnp.dot(ckv_ref[...].astype(BF16), wkv_ref[...], preferred_element_type=F32)
    kn = _seg_rms(kv[:, 0:NW], gkn_ref[...], QK_NOPE, _block_ones(NW, QK_NOPE))
    k_ref[:, 0:NW] = kn.astype(BF16)
    k_ref[:, NW:NW + LANES] = kpe_ref[...].astype(BF16)
    v_ref[...] = kv[:, NW:NW + MLA_HEADS * V_HEAD].astype(BF16)


def _kv_prep(ckv, kpe_t, wkv, gkn, tr):
    r = ckv.shape[0]
    kw = MLA_HEADS * QK_NOPE + LANES
    full = lambda i: (0, 0)
    return pl.pallas_call(
        _kv_prep_kernel,
        grid=(r // tr,),
        in_specs=[pl.BlockSpec((tr, KV_LORA), lambda i: (i, 0)),
                  pl.BlockSpec((tr, LANES), lambda i: (i, 0)),
                  pl.BlockSpec(wkv.shape, full), pl.BlockSpec(gkn.shape, full)],
        out_specs=[pl.BlockSpec((tr, kw), lambda i: (i, 0)),
                   pl.BlockSpec((tr, MIX_C), lambda i: (i, 0))],
        out_shape=[jax.ShapeDtypeStruct((r, kw), BF16), jax.ShapeDtypeStruct((r, MIX_C), BF16)],
        compiler_params=_cparams("parallel"),
        name="kv_prep",
    )(ckv, kpe_t, wkv, gkn)


def _last_key_block(qi, tq, tk, past, lk_valid):
    last_pos = ((past + qi * tq + tq - 1) // CHUNK + 1) * CHUNK - 1
    return jnp.minimum(last_pos, lk_valid - 1) // tk


def _attn_kernel(q_ref, k_ref, v_ref, o_ref, m_ref, l_ref, acc_ref, *, tq, tk, past, lk_valid):
    qi, ki = pl.program_id(1), pl.program_id(2)
    H = MLA_HEADS
    NW = H * QK_NOPE
    QW = NW + LANES
    scale = (QK_NOPE + QK_ROPE) ** -0.5

    @pl.when(ki == 0)
    def _():
        m_ref[...] = jnp.full(m_ref.shape, NEG_BIG, F32)
        l_ref[...] = jnp.zeros(l_ref.shape, F32)
        acc_ref[...] = jnp.zeros(acc_ref.shape, F32)

    @pl.when(ki <= _last_key_block(qi, tq, tk, past, lk_valid))
    def _():
        q = q_ref[...]
        lane = lax.broadcasted_iota(I32, (1, QW), 1)
        head_of_lane = jnp.where(lane < NW, lane // QK_NOPE, (lane - NW) // QK_ROPE)
        qs = jnp.concatenate([jnp.where(head_of_lane == h, q, jnp.zeros_like(q)) for h in range(H)], axis=0)
        s = lax.dot_general(qs, k_ref[...], (((1,), (1,)), ((), ())), preferred_element_type=F32) * scale
        qpos = past + qi * tq + lax.broadcasted_iota(I32, (H * tq, tk), 0) % tq
        kpos = ki * tk + lax.broadcasted_iota(I32, (H * tq, tk), 1)
        visible = jnp.logical_and(kpos // CHUNK <= qpos // CHUNK, kpos < lk_valid)
        s = jnp.where(visible, s, NEG_BIG)
        m_old = m_ref[...]
        m_new = jnp.maximum(m_old, jnp.max(s, axis=-1, keepdims=True))
        alpha = jnp.exp(m_old - m_new)
        p = jnp.where(visible, jnp.exp(s - m_new), 0.0)
        l_ref[...] = alpha * l_ref[...] + jnp.sum(p, axis=-1, keepdims=True)
        acc_ref[...] = alpha * acc_ref[...] + jnp.dot(p.astype(BF16), v_ref[...], preferred_element_type=F32)
        m_ref[...] = m_new

    @pl.when(ki == pl.num_programs(2) - 1)
    def _():
        res = acc_ref[...] / l_ref[...]
        vlane = lax.broadcasted_iota(I32, (1, H * V_HEAD), 1) // V_HEAD
        out = jnp.where(vlane == 0, res[0:tq, :], 0.0)
        for h in range(1, H):
            out = out + jnp.where(vlane == h, res[h * tq:(h + 1) * tq, :], 0.0)
        o_ref[...] = out


def _attention(q3, k3, v3, tq, tk, past, lk_valid):
    b, lq, qw = q3.shape
    lkp = k3.shape[1]
    H = MLA_HEADS
    kmap = lambda i, qi, ki: (i, jnp.minimum(ki, _last_key_block(qi, tq, tk, past, lk_valid)), 0)
    return pl.pallas_call(
        functools.partial(_attn_kernel, tq=tq, tk=tk, past=past, lk_valid=lk_valid),
        grid=(b, lq // tq, lkp // tk),
        in_specs=[pl.BlockSpec((None, tq, qw), lambda i, qi, ki: (i, qi, 0)),
                  pl.BlockSpec((None, tk, qw), kmap),
                  pl.BlockSpec((None, tk, MIX_C), kmap)],
        out_specs=pl.BlockSpec((None, tq, MIX_C), lambda i, qi, ki: (i, qi, 0)),
        out_shape=jax.ShapeDtypeStruct((b, lq, MIX_C), F32),
        scratch_shapes=[pltpu.VMEM((H * tq, 1), F32), pltpu.VMEM((H * tq, 1), F32),
                        pltpu.VMEM((H * tq, MIX_C), F32)],
        compiler_params=_cparams("parallel", "parallel", "arbitrary"),
        name="attention",
    )(q3, k3, v3)


def _out_proj_kernel(a_ref, b_ref, c_ref, x_ref, w_ref, o_ref):
    mixed = jnp.concatenate([a_ref[...].astype(BF16), b_ref[...].astype(BF16), c_ref[...].astype(BF16)], axis=1)
    o_ref[...] = x_ref[...] + jnp.dot(mixed, w_ref[...], preferred_element_type=F32)


def _out_proj(oa, ob, oc, x2d, w_bf16, tm):
    t, d = x2d.shape
    row = lambda i: (i, 0)
    return pl.pallas_call(
        _out_proj_kernel,
        grid=(t // tm,),
        in_specs=[pl.BlockSpec((tm, MIX_A), row), pl.BlockSpec((tm, MIX_B), row), pl.BlockSpec((tm, MIX_C), row),
                  pl.BlockSpec((tm, d), row), pl.BlockSpec(w_bf16.shape, lambda i: (0, 0))],
        out_specs=pl.BlockSpec((tm, d), row),
        out_shape=jax.ShapeDtypeStruct((t, d), F32),
        compiler_params=_cparams("parallel"),
        name="out_proj",
    )(oa, ob, oc, x2d, w_bf16)


def _topk_rows(s, k, payload=None):
    n = s.shape[0]
    ids = lax.broadcasted_iota(I32, s.shape, 0)
    vals, sel = [], []
    for _ in range(k):
        m = jnp.max(s, axis=0, keepdims=True)
        am = jnp.min(jnp.where(s == m, ids, n), axis=0, keepdims=True)
        hit = ids == am
        vals.append(m)
        sel.append(am if payload is None else jnp.sum(jnp.where(hit, payload, 0), axis=0, keepdims=True))
        s = jnp.where(hit, -jnp.inf, s)
    return jnp.concatenate(vals, axis=0), jnp.concatenate(sel, axis=0)


def _router_kernel(x_ref, g_ref, wqt_ref, keys_ref, idx_ref, gate_ref):
    K = PEER_TOPK
    h = _rms(x_ref[...], g_ref[...]).astype(BF16)
    qt = lax.dot_general(wqt_ref[...], h, (((1,), (1,)), ((), ())), preferred_element_type=F32)
    dh = keys_ref.shape[2]
    idx_rows, gate_rows = [], []
    for hd in range(PEER_HEADS):
        tops = []
        for p in range(2):
            r0 = (hd * 2 + p) * dh
            s = jnp.dot(keys_ref[hd * 2 + p], qt[r0:r0 + dh, :].astype(BF16), preferred_element_type=F32)
            tops.append(_topk_rows(s, K))
        (s1, i1), (s2, i2) = tops
        cand = jnp.concatenate([s1[a:a + 1, :] + s2 for a in range(K)], axis=0)
        cidx = jnp.concatenate([i1[a:a + 1, :] * N_KEYS + i2 for a in range(K)], axis=0)
        best, eidx = _topk_rows(cand, K, payload=cidx)
        e = jnp.exp(best - best[0:1, :])
        gate_rows.append(e / jnp.sum(e, axis=0, keepdims=True))
        idx_rows.append(eidx)
    idx_t = jnp.concatenate(idx_rows, axis=0)
    gate_t = jnp.concatenate(gate_rows, axis=0)
    idx_ref[...] = idx_t.astype(F32).T.astype(I32)
    gate_ref[...] = gate_t.T


def _peer_router(x2d, g, wqt, keys, tb):
    t, d = x2d.shape
    return pl.pallas_call(
        _router_kernel,
        grid=(t // tb,),
        in_specs=[pl.BlockSpec((tb, d), lambda i: (i, 0)),
                  pl.BlockSpec((1, d), lambda i: (0, 0)),
                  pl.BlockSpec(wqt.shape, lambda i: (0, 0)),
                  pl.BlockSpec(keys.shape, lambda i: (0, 0, 0))],
        out_specs=[pl.BlockSpec((tb, PEER_SEL), lambda i: (i, 0)),
                   pl.BlockSpec((tb, PEER_SEL), lambda i: (i, 0))],
        out_shape=[jax.ShapeDtypeStruct((t, PEER_SEL), I32), jax.ShapeDtypeStruct((t, PEER_SEL), F32)],
        compiler_params=_cparams("parallel"),
        name="peer_router",
    )(x2d, g, wqt, keys)


def _experts_kernel(x_ref, g_ref, gate_ref, idx_ref, u_hbm, v_hbm, o_ref,
                    idx_smem, ubuf, vbuf, sem_idx, sem_u, sem_v):
    tb = x_ref.shape[0]
    n = PEER_SEL

    cp = pltpu.make_async_copy(idx_ref, idx_smem, sem_idx)
    cp.start()
    cp.wait()

    def rows_copy(tbl, buf, sem, slot, e, j):
        return pltpu.make_async_copy(tbl.at[pl.ds(e, 1)], buf.at[slot, pl.ds(j, 1)], sem.at[slot])

    def issue(t, slot):
        for j in range(n):
            e = idx_smem[t, j]
            rows_copy(u_hbm, ubuf, sem_u, slot, e, j).start()
            rows_copy(v_hbm, vbuf, sem_v, slot, e, j).start()

    def wait_all(slot):
        pltpu.make_async_copy(u_hbm.at[pl.ds(0, n)], ubuf.at[slot], sem_u.at[slot]).wait()
        pltpu.make_async_copy(v_hbm.at[pl.ds(0, n)], vbuf.at[slot], sem_v.at[slot]).wait()

    issue(0, 0)

    def body(t, carry):
        slot = t % 2

        @pl.when(t + 1 < tb)
        def _():
            issue(t + 1, 1 - slot)

        wait_all(slot)
        x_row = x_ref[pl.ds(t, 1), :]
        hn = _rms(x_row, g_ref[...])
        hb = jnp.broadcast_to(hn.astype(BF16), (SUBLANES, hn.shape[1]))
        dots = lax.dot_general(hb, ubuf[slot].astype(BF16), (((1,), (1,)), ((), ())), preferred_element_type=F32)
        d0 = dots[0:1, :]
        act = 0.5 * d0 * (1.0 + lax.erf(d0 * (2.0 ** -0.5)))
        wgt = gate_ref[pl.ds(t, 1), :] * act
        wb = jnp.broadcast_to(wgt.astype(BF16), (SUBLANES, n))
        y = jnp.dot(wb, vbuf[slot].astype(BF16), preferred_element_type=F32)
        o_ref[pl.ds(t, 1), :] = x_row + y[0:1, :]
        return carry

    lax.fori_loop(0, tb, body, 0)


def _peer_experts(x2d, g, gate, idx, u_tbl, v_tbl, tb):
    t, d = x2d.shape
    row = lambda i: (i, 0)
    return pl.pallas_call(
        _experts_kernel,
        grid=(t // tb,),
        in_specs=[pl.BlockSpec((tb, d), row),
                  pl.BlockSpec((1, d), lambda i: (0, 0)),
                  pl.BlockSpec((tb, PEER_SEL), row),
                  pl.BlockSpec((tb, PEER_SEL), row),
                  pl.BlockSpec(memory_space=pl.ANY),
                  pl.BlockSpec(memory_space=pl.ANY)],
        out_specs=pl.BlockSpec((tb, d), row),
        out_shape=jax.ShapeDtypeStruct((t, d), F32),
        scratch_shapes=[pltpu.SMEM((tb, PEER_SEL), I32),
                        pltpu.VMEM((2, PEER_SEL, d), F32),
                        pltpu.VMEM((2, PEER_SEL, d), F32),
                        pltpu.SemaphoreType.DMA(()),
                        pltpu.SemaphoreType.DMA((2,)),
                        pltpu.SemaphoreType.DMA((2,))],
        compiler_params=_cparams("arbitrary"),
        name="peer_experts",
    )(x2d, g, gate, idx, u_tbl, v_tbl)


def _tile_lanes(v, reps):
    return jnp.tile(v.reshape(1, -1), (1, reps)).astype(F32)


def _prep_layer_params(l, p):
    d = p['w_in'].shape[1]
    w = p['w_in'][l]
    o_beta = GDN_QKV + MIX_A
    o_hgrn = o_beta + 2 * GDN_HEADS
    o_mla = o_hgrn + 2 * HGRN_KW + 2 * MIX_B
    n_mla = Q_LORA + KV_LORA + QK_ROPE
    w_all = jnp.concatenate([
        w[:, :o_beta], w[:, o_hgrn:o_mla], w[:, o_mla:o_mla + n_mla],
        jnp.zeros((d, _MLA_W - n_mla), F32), w[:, o_beta:o_hgrn],
        jnp.zeros((d, LANES - 2 * GDN_HEADS), F32)], axis=1).astype(BF16)
    assert w_all.shape[1] == _P_PAD
    gp = jnp.zeros((8, LANES), F32)
    gp = gp.at[0, GDN_HEADS:2 * GDN_HEADS].set(p['gdn_a_log'][l].astype(F32))
    gp = gp.at[1, GDN_HEADS:2 * GDN_HEADS].set(p['gdn_dt_bias'][l].astype(F32))
    convw = jnp.pad(p['conv_w'][l].astype(F32), ((0, 8 - CONV_WIDTH), (0, 0)))
    wq = p['mla_w_q_b'][l].reshape(Q_LORA, MLA_HEADS, QK_NOPE + QK_ROPE)
    wq = jnp.concatenate([wq[:, :, :QK_NOPE].reshape(Q_LORA, -1), wq[:, :, QK_NOPE:].reshape(Q_LORA, -1)], axis=1)
    wkv = p['mla_w_kv_b'][l].reshape(KV_LORA, MLA_HEADS, QK_NOPE + V_HEAD)
    wkv = jnp.concatenate([wkv[:, :, :QK_NOPE].reshape(KV_LORA, -1), wkv[:, :, QK_NOPE:].reshape(KV_LORA, -1)], axis=1)
    keys = p['peer_sub_keys'][l]
    return dict(
        norm_mix=p['norm_mix'][l].reshape(1, d).astype(F32), w_all=w_all, gp=gp, convw=convw,
        gdn_norm=p['gdn_norm'][l].reshape(1, GDN_DV).astype(F32),
        hgrn_norm=_tile_lanes(p['hgrn_norm'][l], HGRN_HEADS),
        wq=wq.astype(BF16), wkv=wkv.astype(BF16),
        gqa=p['mla_q_a_norm'][l].reshape(1, Q_LORA).astype(F32),
        gqn=_tile_lanes(p['mla_q_norm_nope'][l], MLA_HEADS),
        gqr=_tile_lanes(p['mla_q_norm_rope'][l], LANES // QK_ROPE),
        gkv=p['mla_kv_a_norm'][l].reshape(1, KV_LORA).astype(F32),
        gkr=jnp.pad(p['mla_k_norm_rope'][l].reshape(1, QK_ROPE).astype(F32), ((0, 0), (0, LANES - QK_ROPE))),
        gkn=_tile_lanes(p['mla_k_norm_nope'][l], MLA_HEADS),
        w_out=p['w_out'][l].astype(BF16),
        norm_ffn=p['norm_ffn'][l].reshape(1, d).astype(F32),
        wqt=p['peer_w_query'][l].T.astype(BF16),
        keys=keys.reshape(PEER_HEADS * 2, N_KEYS, keys.shape[-1]).astype(BF16),
        u_tbl=p['peer_expert_u'][l], v_tbl=p['peer_expert_v'][l],
    )


def _rope_tables(past, length, rows):
    half = QK_ROPE // 2
    inv = ROPE_THETA ** (-jnp.arange(half, dtype=F32) / half)
    ang = (past + jnp.arange(length, dtype=I32)).astype(F32)[:, None] * inv[None, :]
    reps = LANES // half
    cos = jnp.tile(jnp.cos(ang), (max(rows // length, 1), reps))
    sin = jnp.tile(jnp.sin(ang), (max(rows // length, 1), reps))
    return cos, sin


def _pick(n, prefs):
    for c in prefs:
        if n % c == 0:
            return c
    raise ValueError(f"no block size for {n}")


def _layer(x, past_ckv, past_kpe, gdn_s0, conv_buf, hgrn_s0, lb, lp):
    b, l, d = x.shape
    t = b * l
    past = past_ckv.shape[1]
    x2d = x.reshape(t, d)
    tm = _pick(t, (256, 128, 64, 32))

    proj = _norm_proj(x2d, lp['norm_mix'], lp['w_all'], tm)

    lpad = -(-l // CHUNK) * CHUNK
    valid = l - (lpad - CHUNK)
    proj3 = proj.reshape(b, l, _P_PAD)
    if lpad != l:
        proj3 = jnp.pad(proj3, ((0, 0), (0, lpad - l), (0, 0)))
    conv0 = jnp.pad(conv_buf.astype(F32), ((0, 0), (8 - (CONV_WIDTH - 1), 0), (0, 0)))
    o_a, conv_n, gdn_n = _gdn(proj3, conv0, lp['convw'], lp['gp'], gdn_s0.astype(F32), lp['gdn_norm'], valid)
    eye_h = jnp.eye(HGRN_HEADS, dtype=F32)
    st0 = jnp.einsum('bhde,hg->bhegd', hgrn_s0.astype(F32), eye_h).reshape(b, HGRN_KW, HGRN_KW)
    o_b, st_n = _hgrn(proj3, lb.reshape(1, HGRN_KW).astype(F32), st0, lp['hgrn_norm'], valid)
    st_n = st_n.reshape(b, HGRN_HEADS, HGRN_DV, HGRN_HEADS, HGRN_DK)
    hgrn_n = jnp.stack([st_n[:, h, :, h, :] for h in range(HGRN_HEADS)], axis=1).transpose(0, 1, 3, 2)
    o_a = o_a[:, :l].reshape(t, MIX_A)
    o_b = o_b[:, :l].reshape(t, MIX_B)

    tmm = _pick(t, (256, 128, 64, 32)) if l >= 256 else l
    cos, sin = _rope_tables(past, l, tmm)
    q2d, ckv2d, kpe2d = _mla_prep(proj, cos, sin, lp['wq'], lp['gqa'], lp['gqn'], lp['gqr'], lp['gkv'], lp['gkr'],
                                  tmm)
    ckv = ckv2d.reshape(b, l, KV_LORA)
    kpe_t = kpe2d.reshape(b, l, LANES)
    lk = past + l
    tk = 512
    lkp = -(-lk // tk) * tk
    past_kpe_t = jnp.tile(past_kpe.astype(F32), (1, 1, LANES // QK_ROPE))
    all_ckv = jnp.concatenate([past_ckv.astype(F32), ckv, jnp.zeros((b, lkp - lk, KV_LORA), F32)], axis=1)
    all_kpe = jnp.concatenate([past_kpe_t, kpe_t, jnp.zeros((b, lkp - lk, LANES), F32)], axis=1)
    k2d, v2d = _kv_prep(all_ckv.reshape(b * lkp, KV_LORA), all_kpe.reshape(b * lkp, LANES), lp['wkv'], lp['gkn'], tk)
    tq = _pick(l, (128, 64, 32))
    o_c = _attention(q2d.reshape(b, l, -1), k2d.reshape(b, lkp, -1), v2d.reshape(b, lkp, MIX_C), tq, tk, past, lk)

    x1 = _out_proj(o_a, o_b, o_c.reshape(t, MIX_C), x2d, lp['w_out'], tm)

    tb = _pick(t, (256, 128))
    idx, gate = _peer_router(x1, lp['norm_ffn'], lp['wqt'], lp['keys'], tb)
    x2 = _peer_experts(x1, lp['norm_ffn'], gate, idx, lp['u_tbl'], lp['v_tbl'], _pick(t, (128,)))
    return (x2.reshape(b, l, d), gdn_n, conv_n[:, :CONV_WIDTH - 1], hgrn_n, ckv, kpe_t[:, :, :QK_ROPE])


def kernel(x_prompt, x_sample, state_gdn, state_gdn_conv, state_hgrn, cache_mla_ckv, cache_mla_kpe, norm_mix, w_in, conv_w, gdn_a_log, gdn_dt_bias, gdn_norm, hgrn_lower_bounds, hgrn_norm, mla_q_a_norm, mla_w_q_b, mla_kv_a_norm, mla_w_kv_b, mla_q_norm_nope, mla_q_norm_rope, mla_k_norm_nope, mla_k_norm_rope, w_out, norm_ffn, peer_w_query, peer_sub_keys, peer_expert_u, peer_expert_v):
    dt = x_prompt.dtype
    depth = w_in.shape[0]
    bp = x_prompt.shape[0]
    params = dict(norm_mix=norm_mix, w_in=w_in, conv_w=conv_w, gdn_a_log=gdn_a_log, gdn_dt_bias=gdn_dt_bias,
                  gdn_norm=gdn_norm, hgrn_norm=hgrn_norm, mla_q_a_norm=mla_q_a_norm, mla_w_q_b=mla_w_q_b,
                  mla_kv_a_norm=mla_kv_a_norm, mla_w_kv_b=mla_w_kv_b, mla_q_norm_nope=mla_q_norm_nope,
                  mla_q_norm_rope=mla_q_norm_rope, mla_k_norm_nope=mla_k_norm_nope,
                  mla_k_norm_rope=mla_k_norm_rope, w_out=w_out, norm_ffn=norm_ffn, peer_w_query=peer_w_query,
                  peer_sub_keys=peer_sub_keys, peer_expert_u=peer_expert_u, peer_expert_v=peer_expert_v)
    lb_all = jnp.cumsum(jax.nn.softmax(hgrn_lower_bounds.astype(F32), axis=0), axis=0)
    lb_all = lb_all - lb_all[0:1]

    yp, ys = x_prompt, x_sample
    outs_p, outs_s = [], []
    for l in range(depth):
        lp = _prep_layer_params(l, params)
        yp, *st_p = _layer(
            yp, jnp.zeros((bp, 0, KV_LORA), dt), jnp.zeros((bp, 0, QK_ROPE), dt),
            jnp.zeros((bp, GDN_HEADS, GDN_DK, GDN_DV), F32), jnp.zeros((bp, CONV_WIDTH - 1, GDN_QKV), dt),
            jnp.zeros((bp, HGRN_HEADS, HGRN_DK, HGRN_DV), F32), lb_all[l], lp)
        ys, *st_s = _layer(ys, cache_mla_ckv[l], cache_mla_kpe[l], state_gdn[l], state_gdn_conv[l],
                           state_hgrn[l], lb_all[l], lp)
        outs_p.append([a.astype(dt) for a in st_p])
        outs_s.append([a.astype(dt) for a in st_s])
    stack = lambda outs, i: jnp.stack([o[i] for o in outs])
    return (yp, ys) + tuple(stack(outs_p, i) for i in range(5)) + tuple(stack(outs_s, i) for i in range(5))
```
